```python
import math
import jax
import jax.numpy as jnp
from jax import lax
import numpy as np

D_MODEL = 1024
BATCH = 4
SEQ = 4096
DEPTH = 4
DEC_BATCH = 128
DEC_SEQ = 1
PAST_LEN = 8192
PAGE_SIZE = 128

ROPE_THETA = 10000.0
LN_EPS = 1e-5
NEG_INF = -1e30
ALPHA = (2 * DEPTH) ** 0.25
BETA = (8 * DEPTH) ** -0.25
N_EVEN = (DEPTH + 1) // 2
N_ODD = DEPTH // 2

WINDOW = 128
SWA_HEADS = 8
SWA_KV_HEADS = 2
SWA_HEAD_DIM = 64
SWA_GROUP = SWA_HEADS // SWA_KV_HEADS
RET_HEADS = 8
RET_DK = 64
RET_DV = 64
RET_CHUNK = 128
SWA_Q_W = SWA_HEADS * SWA_HEAD_DIM
SWA_KV_W = SWA_KV_HEADS * SWA_HEAD_DIM
RET_QK_W = RET_HEADS * RET_DK
RET_V_W = RET_HEADS * RET_DV
EVEN_SPLITS = (SWA_Q_W, SWA_KV_W, SWA_KV_W, RET_QK_W, RET_QK_W, RET_V_W, RET_V_W)
EVEN_IN_W = SWA_Q_W + 2 * SWA_KV_W + 2 * RET_QK_W + 2 * RET_V_W
EVEN_OUT_W = SWA_Q_W + RET_V_W
MLA_HEADS = 8
MLA_Q_RANK = 256
MLA_KV_RANK = 128
MLA_NOPE = 64
MLA_ROPE = 32
MLA_V = 64
MLA_IN_W = MLA_Q_RANK + MLA_KV_RANK + MLA_ROPE
MLA_ROW_W = MLA_KV_RANK + MLA_ROPE
Q_BLOCK = 128
N_MEM = 256
MEM_HEADS = 4
MEM_HEAD_DIM = 64
MEM_W = MEM_HEADS * MEM_HEAD_DIM
MOE_GROUPS = 4
MOE_PER_GROUP = 4
MOE_EXPERTS = MOE_GROUPS * MOE_PER_GROUP
MOE_TOP_K = 2
D_EXPERT = 256

kernel_name = 'hybrid_swa_retention_mla_hmoe_step'


def layer_norm(x, g, b):
    xf = x.astype(jnp.float32)
    mu = jnp.mean(xf, axis=-1, keepdims=True)
    var = jnp.mean(jnp.square(xf - mu), axis=-1, keepdims=True)
    return ((xf - mu) * lax.rsqrt(var + LN_EPS) * g + b).astype(x.dtype)


def rms_norm(x, g):
    xf = x.astype(jnp.float32)
    ms = jnp.mean(jnp.square(xf), axis=-1, keepdims=True)
    return (xf * lax.rsqrt(ms + LN_EPS) * g).astype(x.dtype)


def rope(x, pos):
    d = x.shape[-1]
    inv = ROPE_THETA ** (-jnp.arange(0, d, 2, dtype=jnp.float32) / d)
    ang = pos.astype(jnp.float32)[:, None] * inv[None, :]
    cos = jnp.cos(ang)[:, None, :]
    sin = jnp.sin(ang)[:, None, :]
    x1, x2 = jnp.split(x.astype(jnp.float32), 2, axis=-1)
    return jnp.concatenate([x1 * cos - x2 * sin, x1 * sin + x2 * cos], axis=-1).astype(x.dtype)


def sink_softmax(scores, mask, sink):
    s = jnp.where(mask, scores, NEG_INF)
    m = jnp.maximum(jnp.max(s, axis=-1, keepdims=True), sink)
    p = jnp.exp(s - m)
    den = jnp.sum(p, axis=-1, keepdims=True) + jnp.exp(sink - m)
    return p / den


def even_project(x, pos, w_in):
    B, S, _ = x.shape
    offs = [int(o) for o in np.cumsum(EVEN_SPLITS)[:-1]]
    qa, ka, va, qr, kr, vr, g = jnp.split(x @ w_in, offs, axis=-1)
    qa = rope(qa.reshape(B, S, SWA_HEADS, SWA_HEAD_DIM), pos)
    ka = rope(ka.reshape(B, S, SWA_KV_HEADS, SWA_HEAD_DIM), pos)
    va = va.reshape(B, S, SWA_KV_HEADS, SWA_HEAD_DIM)
    qr = rope(qr.reshape(B, S, RET_HEADS, RET_DK), pos)
    kr = rope(kr.reshape(B, S, RET_HEADS, RET_DK), pos) * (RET_DK ** -0.5)
    vr = vr.reshape(B, S, RET_HEADS, RET_DV)
    return qa, ka, va, qr, kr, vr, g


def swa_prompt(q, k, v, sink):
    B, S, H, Dh = q.shape
    nb = S // WINDOW
    qb = q.reshape(B, nb, WINDOW, SWA_KV_HEADS, SWA_GROUP, Dh)
    kb = k.reshape(B, nb, WINDOW, SWA_KV_HEADS, Dh)
    vb = v.reshape(B, nb, WINDOW, SWA_KV_HEADS, Dh)
    kk = jnp.concatenate([jnp.concatenate([jnp.zeros_like(kb[:, :1]), kb[:, :-1]], axis=1), kb], axis=2)
    vv = jnp.concatenate([jnp.concatenate([jnp.zeros_like(vb[:, :1]), vb[:, :-1]], axis=1), vb], axis=2)
    scores = jnp.einsum('bnqkgd,bnskd->bnkgqs', qb, kk).astype(jnp.float32) * (Dh ** -0.5)
    qi = jnp.arange(WINDOW)[:, None]
    si = jnp.arange(2 * WINDOW)[None, :]
    diff = qi + WINDOW - si
    band = (diff >= 0) & (diff < WINDOW)
    valid = (jnp.arange(nb)[:, None, None] > 0) | (si >= WINDOW)[None]
    mask = (band[None] & valid)[None, :, None, None]
    sk = sink.astype(jnp.float32).reshape(SWA_KV_HEADS, SWA_GROUP)[None, None, :, :, None, None]
    p = sink_softmax(scores, mask, sk)
    o = jnp.einsum('bnkgqs,bnskd->bnqkgd', p.astype(vv.dtype), vv)
    return o.reshape(B, S, H * Dh)


def swa_sample(q, k, v, buf_k, buf_v, sink):
    B, T, H, Dh = q.shape
    kk = jnp.concatenate([buf_k, k], axis=1)
    vv = jnp.concatenate([buf_v, v], axis=1)
    qg = q.reshape(B, T, SWA_KV_HEADS, SWA_GROUP, Dh)
    scores = jnp.einsum('bqkgd,bskd->bkgqs', qg, kk).astype(jnp.float32) * (Dh ** -0.5)
    qpos = PAST_LEN + jnp.arange(T)
    kpos = PAST_LEN - WINDOW + jnp.arange(WINDOW + T)
    diff = qpos[:, None] - kpos[None, :]
    mask = (diff >= 0) & (diff < WINDOW)
    sk = sink.astype(jnp.float32).reshape(SWA_KV_HEADS, SWA_GROUP)[None, :, :, None, None]
    p = sink_softmax(scores, mask, sk)
    o = jnp.einsum('bkgqs,bskd->bqkgd', p.astype(vv.dtype), vv).reshape(B, T, H * Dh)
    return o, kk[:, -WINDOW:], vv[:, -WINDOW:]


def ret_log_gamma():
    return jnp.log1p(-jnp.exp2(-5.0 - jnp.arange(RET_HEADS, dtype=jnp.float32)))


def retention_chunk(q, k, v, s0, log_gamma):
    L = q.shape[1]
    idx = jnp.arange(L, dtype=jnp.float32)
    diff = idx[:, None] - idx[None, :]
    intra = jnp.where(diff[None] >= 0, jnp.exp(jnp.maximum(diff, 0.0)[None] * log_gamma[:, None, None]), 0.0).astype(q.dtype)
    q_decay = jnp.exp((idx[:, None] + 1.0) * log_gamma[None, :]).astype(q.dtype)
    k_decay = jnp.exp((L - 1.0 - idx)[:, None] * log_gamma[None, :]).astype(q.dtype)
    s_decay = jnp.exp(L * log_gamma).astype(q.dtype)
    scores = jnp.einsum('blhd,bshd->bhls', q, k) * intra[None]
    o = jnp.einsum('bhls,bshe->blhe', scores, v) + jnp.einsum('blhd,bhde->blhe', q * q_decay[None, :, :, None], s0)
    s_new = s0 * s_decay[None, :, None, None] + jnp.einsum('bshd,bshe->bhde', k * k_decay[None, :, :, None], v)
    return o, s_new


def retention_prompt(q, k, v, log_gamma):
    B, S, H, dk = q.shape
    dv = v.shape[-1]
    nc = S // RET_CHUNK

    def chunks(a):
        return a.reshape(B, nc, RET_CHUNK, H, a.shape[-1]).transpose(1, 0, 2, 3, 4)

    def step(s, xs):
        qc, kc, vc = xs
        o, s = retention_chunk(qc, kc, vc, s, log_gamma)
        return s, o

    s0 = jnp.zeros((B, H, dk, dv), v.dtype)
    s_fin, o = lax.scan(step, s0, (chunks(q), chunks(k), chunks(v)))
    return o.transpose(1, 0, 2, 3, 4).reshape(B, S, H, dv), s_fin


def even_output(o_swa, o_ret, g, gn_g, gn_b, w_out):
    B, S = o_ret.shape[:2]
    of = o_ret.astype(jnp.float32)
    mu = jnp.mean(of, axis=-1, keepdims=True)
    var = jnp.mean(jnp.square(of - mu), axis=-1, keepdims=True)
    on = ((of - mu) * lax.rsqrt(var + LN_EPS)).reshape(B, S, RET_V_W) * gn_g + gn_b
    gated = (on * jax.nn.silu(g.astype(jnp.float32))).astype(o_swa.dtype)
    return jnp.concatenate([o_swa, gated], axis=-1) @ w_out


def mla_project(x, pos, w_in, q_norm, kv_norm, w_uq, w_uk):
    B, S, _ = x.shape
    cq, ckv, kr = jnp.split(x @ w_in, [MLA_Q_RANK, MLA_Q_RANK + MLA_KV_RANK], axis=-1)
    cq = rms_norm(cq, q_norm)
    ckv = rms_norm(ckv, kv_norm)
    kr = rope(kr[:, :, None, :], pos)[:, :, 0]
    q = (cq @ w_uq).reshape(B, S, MLA_HEADS, MLA_NOPE + MLA_ROPE)
    q_nope, q_rope = jnp.split(q, [MLA_NOPE], axis=-1)
    q_rope = rope(q_rope, pos)
    q_lat = jnp.einsum('bshn,chn->bshc', q_nope, w_uk)
    rows = jnp.concatenate([ckv, kr], axis=-1)
    return q_lat, q_rope, rows


def mla_attend(q_lat, q_rope, rows, mask):
    ckv = rows[..., :MLA_KV_RANK]
    kr = rows[..., MLA_KV_RANK:]
    s = (jnp.einsum('bqhc,bkc->bhqk', q_lat, ckv) + jnp.einsum('bqhr,bkr->bhqk', q_rope, kr)).astype(jnp.float32)
    s = jnp.where(mask, s * ((MLA_NOPE + MLA_ROPE) ** -0.5), NEG_INF)
    p = jax.nn.softmax(s, axis=-1).astype(rows.dtype)
    return jnp.einsum('bhqk,bkc->bqhc', p, ckv)


def mla_prompt(q_lat, q_rope, rows):
    B, S = q_lat.shape[:2]
    nq = S // Q_BLOCK

    def blocks(a):
        return a.reshape(B, nq, Q_BLOCK, a.shape[2], a.shape[3]).transpose(1, 0, 2, 3, 4)

    kpos = jnp.arange(S)

    def one(args):
        qlb, qrb, n = args
        qpos = n * Q_BLOCK + jnp.arange(Q_BLOCK)
        return mla_attend(qlb, qrb, rows, kpos[None, :] <= qpos[:, None])

    ctx = lax.map(one, (blocks(q_lat), blocks(q_rope), jnp.arange(nq)))
    return ctx.transpose(1, 0, 2, 3, 4).reshape(B, S, MLA_HEADS, MLA_KV_RANK)


def mla_sample(q_lat, q_rope, rows_new, pool, page_table):
    Bd, T = q_lat.shape[:2]
    past = pool[page_table].reshape(Bd, -1, pool.shape[-1])
    P = past.shape[1]
    allr = jnp.concatenate([past, rows_new], axis=1)
    mask = jnp.arange(P + T)[None, :] <= P + jnp.arange(T)[:, None]
    return mla_attend(q_lat, q_rope, allr, mask)


def mla_output(ctx, w_uv, w_out):
    B, S = ctx.shape[:2]
    o = jnp.einsum('bqhc,chv->bqhv', ctx, w_uv).reshape(B, S, MLA_HEADS * MLA_V)
    return o @ w_out


def mem_kv(mem, w_k, w_v):
    B, M, _ = mem.shape
    return (mem @ w_k).reshape(B, M, MEM_HEADS, MEM_HEAD_DIM), (mem @ w_v).reshape(B, M, MEM_HEADS, MEM_HEAD_DIM)


def mem_attend(x, mk, mv, w_q, w_o):
    B, S, _ = x.shape
    q = (x @ w_q).reshape(B, S, MEM_HEADS, MEM_HEAD_DIM)
    s = jnp.einsum('bqhd,bkhd->bhqk', q, mk).astype(jnp.float32) * (MEM_HEAD_DIM ** -0.5)
    p = jax.nn.softmax(s, axis=-1).astype(mv.dtype)
    o = jnp.einsum('bhqk,bkhd->bqhd', p, mv).reshape(B, S, MEM_W)
    return o @ w_o


def hier_moe(x, w_group, b_group, w_router, b_router, w_in, w_out):
    shp = x.shape
    t = x.reshape(-1, shp[-1])
    n = t.shape[0]
    g_logit = (t @ w_group + b_group).astype(jnp.float32)
    g_prob = jax.nn.softmax(g_logit, axis=-1)
    g_idx = jnp.argmax(g_logit, axis=-1)
    g_gate = jnp.take_along_axis(g_prob, g_idx[:, None], axis=-1)
    e_logit = (t @ w_router + b_router).astype(jnp.float32).reshape(n, MOE_GROUPS, MOE_PER_GROUP)
    e_logit = jnp.take_along_axis(e_logit, g_idx[:, None, None], axis=1)[:, 0]
    top_p, top_i = lax.top_k(jax.nn.softmax(e_logit, axis=-1), MOE_TOP_K)
    top_p = top_p / jnp.sum(top_p, axis=-1, keepdims=True)
    local = jnp.sum(jax.nn.one_hot(top_i, MOE_PER_GROUP, dtype=jnp.float32) * top_p[..., None], axis=1)
    gates = (jax.nn.one_hot(g_idx, MOE_GROUPS, dtype=jnp.float32)[:, :, None] * local[:, None, :] * g_gate[:, :, None]).reshape(n, MOE_EXPERTS)
    h = jnp.einsum('td,edf->tef', t, w_in)
    a, b = jnp.split(h, 2, axis=-1)
    act = jax.nn.silu(a) * b * gates[..., None].astype(h.dtype)
    return jnp.einsum('tef,efd->td', act, w_out).reshape(shp)


def setup_inputs(seed: int = 0) -> dict:
    key = jax.random.key(seed)
    ks = iter(jax.random.split(key, 48))

    def nrm(shape, scale):
        return jax.random.normal(next(ks), shape, jnp.float32) * scale

    n_pages = PAST_LEN // PAGE_SIZE
    n_used = DEC_BATCH * n_pages
    n_pool = n_used + n_used // 4
    page_table = jax.random.permutation(next(ks), n_pool)[:n_used].reshape(DEC_BATCH, n_pages).astype(jnp.int32)
    return {
        'x_prompt': nrm((BATCH, SEQ, D_MODEL), 1.0),
        'x_sample': nrm((DEC_BATCH, DEC_SEQ, D_MODEL), 1.0),
        'mem_prompt': nrm((BATCH, N_MEM, D_MODEL), 1.0),
        'cache_swa_k': nrm((N_EVEN, DEC_BATCH, WINDOW, SWA_KV_HEADS, SWA_HEAD_DIM), 1.0),
        'cache_swa_v': nrm((N_EVEN, DEC_BATCH, WINDOW, SWA_KV_HEADS, SWA_HEAD_DIM), 1.0),
        'state_ret': nrm((N_EVEN, DEC_BATCH, RET_HEADS, RET_DK, RET_DV), 1.0),
        'cache_mla': nrm((N_ODD, n_pool, PAGE_SIZE, MLA_ROW_W), 1.0),
        'page_table': page_table,
        'cache_mem_k': nrm((DEPTH, DEC_BATCH, N_MEM, MEM_HEADS, MEM_HEAD_DIM), 1.0),
        'cache_mem_v': nrm((DEPTH, DEC_BATCH, N_MEM, MEM_HEADS, MEM_HEAD_DIM), 1.0),
        'w_in_even': nrm((N_EVEN, D_MODEL, EVEN_IN_W), D_MODEL ** -0.5),
        'swa_sink': nrm((N_EVEN, SWA_HEADS), 0.5),
        'ret_gn_g': 1.0 + nrm((N_EVEN, RET_V_W), 0.02),
        'ret_gn_b': nrm((N_EVEN, RET_V_W), 0.02),
        'w_out_even': nrm((N_EVEN, EVEN_OUT_W, D_MODEL), BETA * EVEN_OUT_W ** -0.5),
        'w_in_odd': nrm((N_ODD, D_MODEL, MLA_IN_W), D_MODEL ** -0.5),
        'mla_q_norm': 1.0 + nrm((N_ODD, MLA_Q_RANK), 0.02),
        'mla_kv_norm': 1.0 + nrm((N_ODD, MLA_KV_RANK), 0.02),
        'w_uq': nrm((N_ODD, MLA_Q_RANK, MLA_HEADS * (MLA_NOPE + MLA_ROPE)), MLA_Q_RANK ** -0.5),
        'w_uk': nrm((N_ODD, MLA_KV_RANK, MLA_HEADS, MLA_NOPE), MLA_KV_RANK ** -0.5),
        'w_uv': nrm((N_ODD, MLA_KV_RANK, MLA_HEADS, MLA_V), MLA_KV_RANK ** -0.5),
        'w_out_odd': nrm((N_ODD, MLA_HEADS * MLA_V, D_MODEL), BETA * (MLA_HEADS * MLA_V) ** -0.5),
        'w_mem_q': nrm((DEPTH, D_MODEL, MEM_W), D_MODEL ** -0.5),
        'w_mem_k': nrm((DEPTH, D_MODEL, MEM_W), D_MODEL ** -0.5),
        'w_mem_v': nrm((DEPTH, D_MODEL, MEM_W), D_MODEL ** -0.5),
        'w_mem_o': nrm((DEPTH, MEM_W, D_MODEL), BETA * MEM_W ** -0.5),
        'moe_w_group': nrm((DEPTH, D_MODEL, MOE_GROUPS), D_MODEL ** -0.5),
        'moe_b_group': nrm((DEPTH, MOE_GROUPS), 0.01),
        'moe_w_router': nrm((DEPTH, D_MODEL, MOE_EXPERTS), D_MODEL ** -0.5),
        'moe_b_router': nrm((DEPTH, MOE_EXPERTS), 0.01),
        'moe_w_in': nrm((DEPTH, MOE_EXPERTS, D_MODEL, 2 * D_EXPERT), D_MODEL ** -0.5),
        'moe_w_out': nrm((DEPTH, MOE_EXPERTS, D_EXPERT, D_MODEL), BETA * D_EXPERT ** -0.5),
        'ln_g': 1.0 + nrm((DEPTH, 3, D_MODEL), 0.02),
        'ln_b': nrm((DEPTH, 3, D_MODEL), 0.02),
    }


def reference(x_prompt, x_sample, mem_prompt, cache_swa_k, cache_swa_v, state_ret, cache_mla, page_table,
              cache_mem_k, cache_mem_v, w_in_even, swa_sink, ret_gn_g, ret_gn_b, w_out_even,
              w_in_odd, mla_q_norm, mla_kv_norm, w_uq, w_uk, w_uv, w_out_odd,
              w_mem_q, w_mem_k, w_mem_v, w_mem_o, moe_w_group, moe_b_group, moe_w_router, moe_b_router,
              moe_w_in, moe_w_out, ln_g, ln_b):
    pos_p = jnp.arange(x_prompt.shape[1])
    pos_s = PAST_LEN + jnp.arange(x_sample.shape[1])
    log_gamma = ret_log_gamma()
    xp, xs = x_prompt, x_sample
    swa_k_p, swa_v_p, ret_p, mla_p, mem_k_p, mem_v_p = [], [], [], [], [], []
    swa_k_s, swa_v_s, ret_s, mla_s = [], [], [], []
    for l in range(DEPTH):
        i = l // 2
        if l % 2 == 0:
            qa, ka, va, qr, kr, vr, g = even_project(xp, pos_p, w_in_even[i])
            o_a = swa_prompt(qa, ka, va, swa_sink[i])
            o_r, s_fin = retention_prompt(qr, kr, vr, log_gamma)
            mix_p = even_output(o_a, o_r, g, ret_gn_g[i], ret_gn_b[i], w_out_even[i])
            swa_k_p.append(ka[:, -WINDOW:])
            swa_v_p.append(va[:, -WINDOW:])
            ret_p.append(s_fin)
            qa, ka, va, qr, kr, vr, g = even_project(xs, pos_s, w_in_even[i])
            o_a, kbuf, vbuf = swa_sample(qa, ka, va, cache_swa_k[i], cache_swa_v[i], swa_sink[i])
            o_r, s_new = retention_chunk(qr, kr, vr, state_ret[i], log_gamma)
            mix_s = even_output(o_a, o_r, g, ret_gn_g[i], ret_gn_b[i], w_out_even[i])
            swa_k_s.append(kbuf)
            swa_v_s.append(vbuf)
            ret_s.append(s_new)
        else:
            q_lat, q_rope, rows = mla_project(xp, pos_p, w_in_odd[i], mla_q_norm[i], mla_kv_norm[i], w_uq[i], w_uk[i])
            mix_p = mla_output(mla_prompt(q_lat, q_rope, rows), w_uv[i], w_out_odd[i])
            mla_p.append(rows)
            q_lat, q_rope, rows = mla_project(xs, pos_s, w_in_odd[i], mla_q_norm[i], mla_kv_norm[i], w_uq[i], w_uk[i])
            mix_s = mla_output(mla_sample(q_lat, q_rope, rows, cache_mla[i], page_table), w_uv[i], w_out_odd[i])
            mla_s.append(rows)
        xp = layer_norm(ALPHA * xp + mix_p, ln_g[l, 0], ln_b[l, 0])
        xs = layer_norm(ALPHA * xs + mix_s, ln_g[l, 0], ln_b[l, 0])
        mk, mv = mem_kv(mem_prompt, w_mem_k[l], w_mem_v[l])
        mem_k_p.append(mk)
        mem_v_p.append(mv)
        xp = layer_norm(ALPHA * xp + mem_attend(xp, mk, mv, w_mem_q[l], w_mem_o[l]), ln_g[l, 1], ln_b[l, 1])
        xs = layer_norm(ALPHA * xs + mem_attend(xs, cache_mem_k[l], cache_mem_v[l], w_mem_q[l], w_mem_o[l]), ln_g[l, 1], ln_b[l, 1])
        xp = layer_norm(ALPHA * xp + hier_moe(xp, moe_w_group[l], moe_b_group[l], moe_w_router[l], moe_b_router[l], moe_w_in[l], moe_w_out[l]), ln_g[l, 2], ln_b[l, 2])
        xs = layer_norm(ALPHA * xs + hier_moe(xs, moe_w_group[l], moe_b_group[l], moe_w_router[l], moe_b_router[l], moe_w_in[l], moe_w_out[l]), ln_g[l, 2], ln_b[l, 2])
    return (xp, xs, jnp.stack(swa_k_p), jnp.stack(swa_v_p), jnp.stack(ret_p), jnp.stack(mla_p),
            jnp.stack(mem_k_p), jnp.stack(mem_v_p), jnp.stack(swa_k_s), jnp.stack(swa_v_s),
            jnp.stack(ret_s), jnp.stack(mla_s))
```

```python
import functools

import numpy as np
import jax
import jax.numpy as jnp
from jax import lax
from jax.experimental import pallas as pl
from jax.experimental.pallas import tpu as pltpu

F32 = jnp.float32
BF16 = jnp.bfloat16

D_MODEL = 1024
DEPTH = 4
PAST_LEN = 8192
PAGE_SIZE = 128
ROPE_THETA = 10000.0
LN_EPS = 1e-5
NEG_INF = -1e30
ALPHA = (2 * DEPTH) ** 0.25
WINDOW = 128
SWA_HEADS = 8
SWA_KV_HEADS = 2
HEAD_DIM = 64
RET_HEADS = 8
MLA_HEADS = 8
MLA_Q_RANK = 256
MLA_KV_RANK = 128
MLA_NOPE = 64
MLA_ROPE = 32
MLA_V = 64
MLA_ROW_W = MLA_KV_RANK + MLA_ROPE
N_MEM = 256
MEM_HEADS = 4
MEM_W = MEM_HEADS * HEAD_DIM
MOE_GROUPS = 4
MOE_PER_GROUP = 4
MOE_EXPERTS = 16
D_EXPERT = 256

LANES = 128
MLA_QK_PAD = 2 * LANES
ROUTER_LANE0 = 16
VMEM_LIMIT = 56 * 1024 * 1024


def _cparams(*sem):
    return pltpu.CompilerParams(dimension_semantics=sem, vmem_limit_bytes=VMEM_LIMIT)


def _dot(a, b):
    return jnp.dot(a.astype(BF16), b.astype(BF16), preferred_element_type=F32)


def _dot_nt(a, b):
    return lax.dot_general(a.astype(BF16), b.astype(BF16), (((1,), (1,)), ((), ())),
                           preferred_element_type=F32)


def _layer_norm(z, g, b):
    mu = jnp.mean(z, axis=-1, keepdims=True)
    d = z - mu
    var = jnp.mean(d * d, axis=-1, keepdims=True)
    return d * lax.rsqrt(var + LN_EPS) * g + b


def _silu(x):
    return x * (1.0 / (1.0 + jnp.exp(-x)))


def _rope_tile(t, cos, sin_signed, half):
    lane = lax.broadcasted_iota(jnp.int32, t.shape, 1)
    first = (lane % (2 * half)) < half
    partner = jnp.where(first, pltpu.roll(t, LANES - half, 1), pltpu.roll(t, half, 1))
    return t * cos + partner * sin_signed


def _rope_tables(pos, head_dim):
    inv = ROPE_THETA ** (-jnp.arange(0, head_dim, 2, dtype=jnp.float32) / head_dim)
    ang = pos.astype(jnp.float32)[:, None] * inv[None, :]
    cos = jnp.cos(ang)
    sin = jnp.sin(ang)
    reps = LANES // head_dim
    cos_t = jnp.tile(jnp.concatenate([cos, cos], axis=-1), (1, reps))
    sin_t = jnp.tile(jnp.concatenate([-sin, sin], axis=-1), (1, reps))
    return cos_t, sin_t


def _lo_mask(shape):
    return lax.broadcasted_iota(jnp.int32, shape, len(shape) - 1) < HEAD_DIM


EVEN_W = 5 * 512 + 256


def _even_proj_kernel(x_ref, w_ref, cos_ref, sin_ref, qa_ref, qr_ref, kr_ref, vr_ref, g_ref, kava_ref):
    x = x_ref[...].astype(BF16)
    cos = cos_ref[...]
    sin = sin_ref[...]

    def seg(off, width):
        return jnp.dot(x, w_ref[:, off:off + width], preferred_element_type=F32)

    def rope_store(dst, y, scale):
        for c in range(y.shape[1] // LANES):
            r = _rope_tile(y[:, c * LANES:(c + 1) * LANES], cos, sin, HEAD_DIM // 2)
            if scale is not None:
                r = r * scale
            dst[:, c * LANES:(c + 1) * LANES] = r

    rope_store(qa_ref, seg(0, 512), None)
    rope_store(qr_ref, seg(512, 512), None)
    rope_store(kr_ref, seg(1024, 512), HEAD_DIM ** -0.5)
    vr_ref[...] = seg(1536, 512)
    g_ref[...] = seg(2048, 512)
    kv = seg(2560, 256)
    kava_ref[:, 0:LANES] = _rope_tile(kv[:, 0:LANES], cos, sin, HEAD_DIM // 2)
    kava_ref[:, LANES:2 * LANES] = kv[:, LANES:2 * LANES]


def _even_proj(x, w, cos, sin, tm):
    n = x.shape[0]
    nper = cos.shape[0] // tm
    row = lambda i: (i, 0)
    out_shapes = [jax.ShapeDtypeStruct((n, 512), F32)] * 5 + [jax.ShapeDtypeStruct((n, 256), F32)]
    return pl.pallas_call(
        _even_proj_kernel,
        grid=(n // tm,),
        in_specs=[pl.BlockSpec((tm, D_MODEL), row),
                  pl.BlockSpec((D_MODEL, EVEN_W), lambda i: (0, 0)),
                  pl.BlockSpec((tm, LANES), lambda i: (i % nper, 0)),
                  pl.BlockSpec((tm, LANES), lambda i: (i % nper, 0))],
        out_specs=[pl.BlockSpec((tm, 512), row)] * 5 + [pl.BlockSpec((tm, 256), row)],
        out_shape=out_shapes,
        compiler_params=_cparams("parallel"),
        name="even_proj",
    )(x, w, cos, sin)


def _sink_softmax(s, mask, sink):
    s = jnp.where(mask, s, NEG_INF)
    m = jnp.maximum(jnp.max(s, axis=-1, keepdims=True), sink)
    p = jnp.exp(s - m)
    den = jnp.sum(p, axis=-1, keepdims=True) + jnp.exp(sink - m)
    return p * (1.0 / den)


def _group_norm_pair(o, lo):
    inv = 1.0 / HEAD_DIM
    s_lo = jnp.sum(jnp.where(lo, o, 0.0), axis=-1, keepdims=True)
    s_hi = jnp.sum(jnp.where(lo, 0.0, o), axis=-1, keepdims=True)
    d = o - jnp.where(lo, s_lo, s_hi) * inv
    d2 = d * d
    v_lo = jnp.sum(jnp.where(lo, d2, 0.0), axis=-1, keepdims=True)
    v_hi = jnp.sum(jnp.where(lo, 0.0, d2), axis=-1, keepdims=True)
    return d * lax.rsqrt(jnp.where(lo, v_lo, v_hi) * inv + LN_EPS)


def _even_mix_prompt_kernel(sink_ref, qa_ref, kvc_ref, kvp_ref, qr_ref, kr_ref, vr_ref, g_ref,
                            intra_ref, qdec_ref, kdec_ref, sdec_ref, gng_ref, gnb_ref,
                            mix_ref, st_ref, state):
    n = pl.program_id(1)
    w = WINDOW
    lo = _lo_mask((w, LANES))
    lo2 = _lo_mask((2 * w, LANES))

    kvc = kvc_ref[...]
    kvp = kvp_ref[...]
    k = jnp.concatenate([kvp[:, 0:LANES], kvc[:, 0:LANES]], axis=0)
    v = jnp.concatenate([kvp[:, LANES:2 * LANES], kvc[:, LANES:2 * LANES]], axis=0)
    k_sw = pltpu.roll(k, HEAD_DIM, 1)
    v_sw = pltpu.roll(v, HEAD_DIM, 1)
    qi = lax.broadcasted_iota(jnp.int32, (w, 2 * w), 0)
    si = lax.broadcasted_iota(jnp.int32, (w, 2 * w), 1)
    diff = qi + w - si
    first_key = jnp.where(n > 0, 0, w)
    mask = (diff >= 0) & (diff < w) & (si >= first_key)
    for j in range(SWA_KV_HEADS):
        k_lo = jnp.where(lo2, k if j == 0 else k_sw, 0.0)
        k_hi = jnp.where(lo2, 0.0, k_sw if j == 0 else k)
        v_lo = jnp.where(lo2, v if j == 0 else v_sw, 0.0)
        v_hi = jnp.where(lo2, 0.0, v_sw if j == 0 else v)
        kcat = jnp.concatenate([k_lo, k_hi], axis=0).astype(BF16)
        vcat = jnp.concatenate([v_lo, v_hi], axis=0).astype(BF16)
        for t in range(2 * j, 2 * j + 2):
            q = qa_ref[:, t * LANES:(t + 1) * LANES]
            s = _dot_nt(q, kcat) * (HEAD_DIM ** -0.5)
            p_a = _sink_softmax(s[:, 0:2 * w], mask, sink_ref[2 * t])
            p_b = _sink_softmax(s[:, 2 * w:4 * w], mask, sink_ref[2 * t + 1])
            p = jnp.concatenate([p_a, p_b], axis=1)
            mix_ref[:, t * LANES:(t + 1) * LANES] = _dot(p, vcat)

    @pl.when(n == 0)
    def _():
        state[...] = jnp.zeros_like(state)

    row = lax.broadcasted_iota(jnp.int32, (LANES, LANES), 0)
    col = lax.broadcasted_iota(jnp.int32, (LANES, LANES), 1)
    diag = (row < HEAD_DIM) == (col < HEAD_DIM)
    for t in range(RET_HEADS // 2):
        sl = slice(t * LANES, (t + 1) * LANES)
        q = qr_ref[:, sl]
        kk = kr_ref[:, sl]
        vv = vr_ref[:, sl]
        s_a = _dot_nt(jnp.where(lo, q, 0.0), kk) * intra_ref[2 * t]
        s_b = _dot_nt(jnp.where(lo, 0.0, q), kk) * intra_ref[2 * t + 1]
        vcat = jnp.concatenate([jnp.where(lo, vv, 0.0), jnp.where(lo, 0.0, vv)], axis=0)
        s0 = state[t]
        o = _dot(jnp.concatenate([s_a, s_b], axis=1), vcat) + _dot(q * qdec_ref[:, sl], s0)
        state[t] = s0 * sdec_ref[:, sl] + jnp.where(diag, _dot((kk * kdec_ref[:, sl]).T, vv), 0.0)
        on = _group_norm_pair(o, lo) * gng_ref[:, sl] + gnb_ref[:, sl]
        mix_ref[:, 512 + t * LANES:512 + (t + 1) * LANES] = on * _silu(g_ref[:, sl])

    @pl.when(n == pl.num_programs(1) - 1)
    def _():
        for t in range(RET_HEADS // 2):
            s_fin = state[t]
            st_ref[0, 2 * t] = s_fin[0:HEAD_DIM, 0:HEAD_DIM]
            st_ref[0, 2 * t + 1] = s_fin[HEAD_DIM:LANES, HEAD_DIM:LANES]


def _even_mix_prompt(sink, qa, kava, qr, kr, vr, g, intra, qdec, kdec, sdec, gng, gnb, batch):
    n = qa.shape[0]
    nb = n // batch // WINDOW
    cur = lambda b, i: (b * nb + i, 0)
    prev = lambda b, i: (b * nb + jnp.maximum(i - 1, 0), 0)
    const2 = lambda b, i: (0, 0)
    blk = pl.BlockSpec((WINDOW, 512), cur)
    return pl.pallas_call(
        _even_mix_prompt_kernel,
        grid=(batch, nb),
        in_specs=[pl.BlockSpec(memory_space=pltpu.SMEM),
                  blk,
                  pl.BlockSpec((WINDOW, 256), cur),
                  pl.BlockSpec((WINDOW, 256), prev),
                  blk, blk, blk, blk,
                  pl.BlockSpec((RET_HEADS, WINDOW, WINDOW), lambda b, i: (0, 0, 0)),
                  pl.BlockSpec((WINDOW, 512), const2),
                  pl.BlockSpec((WINDOW, 512), const2),
                  pl.BlockSpec((1, 512), const2),
                  pl.BlockSpec((1, 512), const2),
                  pl.BlockSpec((1, 512), const2)],
        out_specs=[pl.BlockSpec((WINDOW, D_MODEL), cur),
                   pl.BlockSpec((1, RET_HEADS, HEAD_DIM, HEAD_DIM), lambda b, i: (b, 0, 0, 0))],
        out_shape=[jax.ShapeDtypeStruct((n, D_MODEL), F32),
                   jax.ShapeDtypeStruct((batch, RET_HEADS, HEAD_DIM, HEAD_DIM), F32)],
        scratch_shapes=[pltpu.VMEM((RET_HEADS // 2, LANES, LANES), F32)],
        compiler_params=_cparams("arbitrary", "arbitrary"),
        name="even_mix_prompt",
    )(sink, qa, kava, kava, qr, kr, vr, g, intra, qdec, kdec, sdec, gng, gnb)


SAMPLE_BLOCK = 8


def _even_mix_sample_kernel(gamma_ref, sink_ref, qa_ref, kava_ref, ck_ref, cv_ref,
                            qc_ref, kc_ref, vr_ref, g_ref, st_ref, gng_ref, gnb_ref,
                            oswa_ref, gated_ref, ck_out, cv_out, st_out):
    w = WINDOW
    hrow = lax.broadcasted_iota(jnp.int32, (SWA_HEADS, LANES), 0)
    lane = lax.broadcasted_iota(jnp.int32, (SWA_HEADS, LANES), 1)
    own_half = (hrow < SWA_HEADS // 2) == (lane < HEAD_DIM)
    in_window = lane >= 1
    first_kv = lax.broadcasted_iota(jnp.int32, (SWA_HEADS, HEAD_DIM), 0) < SWA_HEADS // 2
    sink = sink_ref[...]
    for bb in range(SAMPLE_BLOCK):
        q8 = qa_ref[bb]
        qpad = jnp.where(own_half, jnp.concatenate([q8, q8], axis=1), 0.0)
        kbuf = ck_ref[bb]
        vbuf = cv_ref[bb]
        knew = kava_ref[bb:bb + 1, 0:LANES]
        vnew = kava_ref[bb:bb + 1, LANES:2 * LANES]
        scale = HEAD_DIM ** -0.5
        s = jnp.where(in_window, _dot_nt(qpad, kbuf) * scale, NEG_INF)
        s_new = jnp.sum(qpad * knew, axis=-1, keepdims=True) * scale
        m = jnp.maximum(jnp.maximum(jnp.max(s, axis=-1, keepdims=True), s_new), sink)
        p = jnp.exp(s - m)
        p_new = jnp.exp(s_new - m)
        den = jnp.sum(p, axis=-1, keepdims=True) + p_new + jnp.exp(sink - m)
        inv = 1.0 / den
        o = _dot(p * inv, vbuf) + (p_new * inv) * vnew
        oswa_ref[bb] = jnp.where(first_kv, o[:, 0:HEAD_DIM], o[:, HEAD_DIM:LANES])
        ck_out[bb, 0:w - 1, :] = ck_ref[bb, 1:w, :]
        ck_out[bb, w - 1:w, :] = knew
        cv_out[bb, 0:w - 1, :] = cv_ref[bb, 1:w, :]
        cv_out[bb, w - 1:w, :] = vnew
        for h in range(RET_HEADS):
            gam = gamma_ref[h]
            s0 = st_ref[bb, h]
            qc = qc_ref[bb, h]
            kc = kc_ref[bb, h]
            v = vr_ref[bb, h]
            qk = jnp.sum(qc * kc, axis=0, keepdims=True)
            o_r = qk * v + jnp.sum((qc * gam) * s0, axis=0, keepdims=True)
            st_out[bb, h] = s0 * gam + kc * v
            mu = jnp.mean(o_r, axis=-1, keepdims=True)
            d = o_r - mu
            var = jnp.mean(d * d, axis=-1, keepdims=True)
            on = d * lax.rsqrt(var + LN_EPS) * gng_ref[h] + gnb_ref[h]
            gated_ref[bb, h] = on * _silu(g_ref[bb, h])


def _even_mix_sample(layer, gamma, sink, qa, kava, cache_k, cache_v, qr, kr, vr, g, state, gng, gnb):
    nd = qa.shape[0]
    nb = SAMPLE_BLOCK
    h, d = RET_HEADS, HEAD_DIM
    b3 = lambda i: (i, 0, 0)
    b4 = lambda i: (i, 0, 0, 0)
    return pl.pallas_call(
        _even_mix_sample_kernel,
        grid=(nd // nb,),
        in_specs=[pl.BlockSpec(memory_space=pltpu.SMEM),
                  pl.BlockSpec((SWA_HEADS, 1), lambda i: (0, 0)),
                  pl.BlockSpec((nb, SWA_HEADS, d), b3),
                  pl.BlockSpec((nb, 256), lambda i: (i, 0)),
                  pl.BlockSpec((None, nb, WINDOW, LANES), lambda i: (layer, i, 0, 0)),
                  pl.BlockSpec((None, nb, WINDOW, LANES), lambda i: (layer, i, 0, 0)),
                  pl.BlockSpec((nb, h, d, 1), b4),
                  pl.BlockSpec((nb, h, d, 1), b4),
                  pl.BlockSpec((nb, h, 1, d), b4),
                  pl.BlockSpec((nb, h, 1, d), b4),
                  pl.BlockSpec((None, nb, h, d, d), lambda i: (layer, i, 0, 0, 0)),
                  pl.BlockSpec((h, 1, d), lambda i: (0, 0, 0)),
                  pl.BlockSpec((h, 1, d), lambda i: (0, 0, 0))],
        out_specs=[pl.BlockSpec((nb, SWA_HEADS, d), b3),
                   pl.BlockSpec((nb, h, 1, d), b4),
                   pl.BlockSpec((nb, WINDOW, LANES), b3),
                   pl.BlockSpec((nb, WINDOW, LANES), b3),
                   pl.BlockSpec((nb, h, d, d), b4)],
        out_shape=[jax.ShapeDtypeStruct((nd, SWA_HEADS, d), F32),
                   jax.ShapeDtypeStruct((nd, h, 1, d), F32),
                   jax.ShapeDtypeStruct((nd, WINDOW, LANES), F32),
                   jax.ShapeDtypeStruct((nd, WINDOW, LANES), F32),
                   jax.ShapeDtypeStruct((nd, h, d, d), F32)],
        compiler_params=_cparams("parallel"),
        name="even_mix_sample",
    )(gamma, sink, qa.reshape(nd, SWA_HEADS, d), kava, cache_k, cache_v,
      qr.reshape(nd, h, d, 1), kr.reshape(nd, h, d, 1), vr.reshape(nd, h, 1, d), g.reshape(nd, h, 1, d),
      state, gng.reshape(h, 1, d), gnb.reshape(h, 1, d))


def _out_proj_kernel(n_a, has_pre, *refs):
    x_ref = refs[0]
    a_refs = refs[1:1 + n_a]
    pos = 1 + n_a
    pre_ref = refs[pos] if has_pre else None
    pos += int(has_pre)
    w_refs = refs[pos:pos + n_a]
    g_ref, b_ref, o_ref = refs[pos + n_a:pos + n_a + 3]
    y = None
    for idx in range(n_a):
        a = a_refs[idx][...]
        if has_pre and idx == 0:
            a = _dot(a, pre_ref[...])
        part = _dot(a, w_refs[idx][...])
        y = part if y is None else y + part
    o_ref[...] = _layer_norm(ALPHA * x_ref[...] + y, g_ref[...], b_ref[...])


def _out_proj_ln(x, a_list, w_list, g, b, tm, w_pre=None):
    n = x.shape[0]
    row = lambda i: (i, 0)
    const = lambda i: (0, 0)
    in_specs = [pl.BlockSpec((tm, D_MODEL), row)]
    in_specs += [pl.BlockSpec((tm, a.shape[1]), row) for a in a_list]
    args = [x] + list(a_list)
    if w_pre is not None:
        in_specs.append(pl.BlockSpec(w_pre.shape, const))
        args.append(w_pre)
    in_specs += [pl.BlockSpec(w.shape, const) for w in w_list]
    args += list(w_list)
    in_specs += [pl.BlockSpec((1, D_MODEL), const)] * 2
    args += [g.reshape(1, D_MODEL), b.reshape(1, D_MODEL)]
    return pl.pallas_call(
        functools.partial(_out_proj_kernel, len(a_list), w_pre is not None),
        grid=(n // tm,),
        in_specs=in_specs,
        out_specs=pl.BlockSpec((tm, D_MODEL), row),
        out_shape=jax.ShapeDtypeStruct((n, D_MODEL), F32),
        compiler_params=_cparams("parallel"),
        name="out_proj_ln",
    )(*args)


def _mla_proj_kernel(x_ref, win_ref, qn_ref, kvn_ref, wuqn_ref, wuqr_ref, wuk_ref, cos_ref, sin_ref,
                     q_ref, rows_ref, rowsk_ref):
    cos = cos_ref[...]
    sin = sin_ref[...]
    y = _dot(x_ref[...], win_ref[...])
    cq = y[:, 0:MLA_Q_RANK]
    ckv = y[:, MLA_Q_RANK:MLA_Q_RANK + MLA_KV_RANK]
    kr = y[:, MLA_Q_RANK + MLA_KV_RANK:MLA_Q_RANK + MLA_KV_RANK + LANES]
    cq = cq * lax.rsqrt(jnp.mean(cq * cq, axis=-1, keepdims=True) + LN_EPS) * qn_ref[...]
    ckv = ckv * lax.rsqrt(jnp.mean(ckv * ckv, axis=-1, keepdims=True) + LN_EPS) * kvn_ref[...]
    kr = _rope_tile(kr, cos, sin, MLA_ROPE // 2)
    rows_ref[:, 0:MLA_KV_RANK] = ckv
    rows_ref[:, MLA_KV_RANK:MLA_ROW_W] = kr[:, 0:MLA_ROPE]
    rowsk_ref[:, 0:LANES] = ckv.astype(BF16)
    rowsk_ref[:, LANES:2 * LANES] = kr.astype(BF16)
    cqb = cq.astype(BF16)
    q_lat = _dot(jnp.dot(cqb, wuqn_ref[...], preferred_element_type=F32), wuk_ref[...])
    q_rope = jnp.dot(cqb, wuqr_ref[...], preferred_element_type=F32)
    scale = (MLA_NOPE + MLA_ROPE) ** -0.5
    for h in range(MLA_HEADS):
        sl = slice(h * LANES, (h + 1) * LANES)
        q_ref[:, h * MLA_QK_PAD:h * MLA_QK_PAD + LANES] = (q_lat[:, sl] * scale).astype(BF16)
        q_ref[:, h * MLA_QK_PAD + LANES:(h + 1) * MLA_QK_PAD] = (
            _rope_tile(q_rope[:, sl], cos, sin, MLA_ROPE // 2) * scale).astype(BF16)


def _mla_proj(x, win, qn, kvn, wuqn, wuqr, wukbd, cos, sin, tm):
    n = x.shape[0]
    nper = cos.shape[0] // tm
    row = lambda i: (i, 0)
    const = lambda i: (0, 0)
    return pl.pallas_call(
        _mla_proj_kernel,
        grid=(n // tm,),
        in_specs=[pl.BlockSpec((tm, D_MODEL), row),
                  pl.BlockSpec(win.shape, const),
                  pl.BlockSpec((1, MLA_Q_RANK), const),
                  pl.BlockSpec((1, MLA_KV_RANK), const),
                  pl.BlockSpec(wuqn.shape, const),
                  pl.BlockSpec(wuqr.shape, const),
                  pl.BlockSpec(wukbd.shape, const),
                  pl.BlockSpec((tm, LANES), lambda i: (i % nper, 0)),
                  pl.BlockSpec((tm, LANES), lambda i: (i % nper, 0))],
        out_specs=[pl.BlockSpec((tm, MLA_HEADS * MLA_QK_PAD), row),
                   pl.BlockSpec((tm, MLA_ROW_W), row),
                   pl.BlockSpec((tm, MLA_QK_PAD), row)],
        out_shape=[jax.ShapeDtypeStruct((n, MLA_HEADS * MLA_QK_PAD), BF16),
                   jax.ShapeDtypeStruct((n, MLA_ROW_W), F32),
                   jax.ShapeDtypeStruct((n, MLA_QK_PAD), BF16)],
        compiler_params=_cparams("parallel"),
        name="mla_proj",
    )(x, win, qn.reshape(1, -1), kvn.reshape(1, -1), wuqn, wuqr, wukbd, cos, sin)


MLA_TQ = 128
MLA_TK = 256


def _mla_flash_kernel(q_ref, k_ref, o_ref, m_sc, l_sc, acc_sc):
    i = pl.program_id(1)
    rows = MLA_TQ * MLA_HEADS
    q = q_ref[...]
    m_sc[...] = jnp.full_like(m_sc, NEG_INF)
    l_sc[...] = jnp.zeros_like(l_sc)
    acc_sc[...] = jnp.zeros_like(acc_sc)

    def step(j, masked):
        kblk = k_ref[pl.ds(pl.multiple_of(j * MLA_TK, MLA_TK), MLA_TK), :]
        s = lax.dot_general(q, kblk, (((1,), (1,)), ((), ())), preferred_element_type=F32)
        if masked:
            tok = i * MLA_TQ + lax.broadcasted_iota(jnp.int32, (rows, MLA_TK), 0) // MLA_HEADS
            key = j * MLA_TK + lax.broadcasted_iota(jnp.int32, (rows, MLA_TK), 1)
            s = jnp.where(key <= tok, s, NEG_INF)
        m_prev = m_sc[...]
        m_new = jnp.maximum(m_prev, jnp.max(s, axis=-1, keepdims=True))
        alpha = jnp.exp(m_prev - m_new)
        p = jnp.exp(s - m_new)
        l_sc[...] = alpha * l_sc[...] + jnp.sum(p, axis=-1, keepdims=True)
        acc_sc[...] = alpha * acc_sc[...] + jnp.dot(p.astype(BF16), kblk[:, 0:MLA_KV_RANK],
                                                    preferred_element_type=F32)
        m_sc[...] = m_new

    n_full = (i * MLA_TQ) // MLA_TK

    def body(j, carry):
        step(j, False)
        return carry

    lax.fori_loop(0, n_full, body, 0)
    step(n_full, True)
    o_ref[...] = acc_sc[...] * (1.0 / l_sc[...])


def _mla_flash(q2d, rowsk, batch, seq):
    rows = MLA_TQ * MLA_HEADS
    nq = seq // MLA_TQ
    return pl.pallas_call(
        _mla_flash_kernel,
        grid=(batch, nq),
        in_specs=[pl.BlockSpec((rows, MLA_QK_PAD), lambda b, i: (b * nq + i, 0)),
                  pl.BlockSpec((seq, MLA_QK_PAD), lambda b, i: (b, 0))],
        out_specs=pl.BlockSpec((rows, MLA_KV_RANK), lambda b, i: (b * nq + i, 0)),
        out_shape=jax.ShapeDtypeStruct((batch * seq * MLA_HEADS, MLA_KV_RANK), F32),
        scratch_shapes=[pltpu.VMEM((rows, 1), F32), pltpu.VMEM((rows, 1), F32),
                        pltpu.VMEM((rows, MLA_KV_RANK), F32)],
        compiler_params=_cparams("parallel", "arbitrary"),
        name="mla_flash",
    )(q2d, rowsk)


N_PAGES = PAST_LEN // PAGE_SIZE


def _mla_sample_kernel(layer, pt_ref, q_ref, new_ref, cache_ref, o_ref, buf, sem):
    b = pl.program_id(0)
    nb = pl.num_programs(0)
    slot = b % 2

    def page_copy(bb, sl, p):
        return pltpu.make_async_copy(
            cache_ref.at[layer, pt_ref[bb, p]],
            buf.at[sl, pl.ds(p * PAGE_SIZE, PAGE_SIZE), :],
            sem.at[sl])

    def start_fetch(bb, sl):
        for p in range(N_PAGES):
            page_copy(bb, sl, p).start()

    @pl.when(b == 0)
    def _():
        start_fetch(0, 0)

    @pl.when(b + 1 < nb)
    def _():
        start_fetch(b + 1, 1 - slot)

    for p in range(N_PAGES):
        page_copy(b, slot, p).wait()

    lat = buf[slot, :, 0:MLA_KV_RANK].astype(BF16)
    rope = buf[slot, :, MLA_KV_RANK:MLA_ROW_W].astype(BF16)
    q = q_ref[0]
    new = new_ref[0]
    nt = (((1,), (1,)), ((), ()))
    s = (lax.dot_general(q[:, 0:MLA_KV_RANK], lat, nt, preferred_element_type=F32)
         + lax.dot_general(q[:, MLA_KV_RANK:MLA_ROW_W], rope, nt, preferred_element_type=F32))
    s_new = jnp.sum(q.astype(F32) * new.astype(F32), axis=-1, keepdims=True)
    m = jnp.maximum(jnp.max(s, axis=-1, keepdims=True), s_new)
    p = jnp.exp(s - m)
    p_new = jnp.exp(s_new - m)
    inv = 1.0 / (jnp.sum(p, axis=-1, keepdims=True) + p_new)
    ctx = jnp.dot((p * inv).astype(BF16), lat, preferred_element_type=F32)
    o_ref[0] = ctx + (p_new * inv) * new[:, 0:MLA_KV_RANK].astype(F32)


def _mla_sample(layer, page_table, q2d, rowsk, cache):
    nd = page_table.shape[0]
    grid_spec = pltpu.PrefetchScalarGridSpec(
        num_scalar_prefetch=1,
        grid=(nd,),
        in_specs=[pl.BlockSpec((1, MLA_HEADS, MLA_QK_PAD), lambda b, pt: (b, 0, 0)),
                  pl.BlockSpec((1, 1, MLA_QK_PAD), lambda b, pt: (b, 0, 0)),
                  pl.BlockSpec(memory_space=pl.ANY)],
        out_specs=pl.BlockSpec((1, MLA_HEADS, MLA_KV_RANK), lambda b, pt: (b, 0, 0)),
        scratch_shapes=[pltpu.VMEM((2, PAST_LEN, MLA_ROW_W), F32),
                        pltpu.SemaphoreType.DMA((2,))],
    )
    return pl.pallas_call(
        functools.partial(_mla_sample_kernel, layer),
        grid_spec=grid_spec,
        out_shape=jax.ShapeDtypeStruct((nd, MLA_HEADS, MLA_KV_RANK), F32),
        compiler_params=_cparams("arbitrary"),
        name="mla_sample",
    )(page_table, q2d.reshape(nd, MLA_HEADS, MLA_QK_PAD), rowsk.reshape(nd, 1, MLA_QK_PAD), cache)


def _linear_kernel(x_ref, w_ref, o_ref):
    o_ref[...] = _dot(x_ref[...], w_ref[...])


def _linear(x, w, tm):
    n, k = x.shape
    m = w.shape[1]
    return pl.pallas_call(
        _linear_kernel,
        grid=(n // tm,),
        in_specs=[pl.BlockSpec((tm, k), lambda i: (i, 0)), pl.BlockSpec((k, m), lambda i: (0, 0))],
        out_specs=pl.BlockSpec((tm, m), lambda i: (i, 0)),
        out_shape=jax.ShapeDtypeStruct((n, m), F32),
        compiler_params=_cparams("parallel"),
        name="linear",
    )(x, w)


def _mem_attn_prompt_kernel(x_ref, wq_ref, kv_ref, wo_ref, g_ref, b_ref, o_ref):
    x = x_ref[...]
    tm = x.shape[0]
    q = _dot(x, wq_ref[...])
    kv = kv_ref[...]
    lo = _lo_mask((N_MEM, LANES))
    outs = []
    for t in range(MEM_HEADS // 2):
        kp = kv[:, t * LANES:(t + 1) * LANES]
        vp = kv[:, MEM_W + t * LANES:MEM_W + (t + 1) * LANES]
        kcat = jnp.concatenate([jnp.where(lo, kp, 0.0), jnp.where(lo, 0.0, kp)], axis=0)
        vcat = jnp.concatenate([jnp.where(lo, vp, 0.0), jnp.where(lo, 0.0, vp)], axis=0)
        s = _dot_nt(q[:, t * LANES:(t + 1) * LANES], kcat) * (HEAD_DIM ** -0.5)
        ps = []
        for half in range(2):
            sh = s[:, half * N_MEM:(half + 1) * N_MEM]
            e = jnp.exp(sh - jnp.max(sh, axis=-1, keepdims=True))
            ps.append(e * (1.0 / jnp.sum(e, axis=-1, keepdims=True)))
        outs.append(_dot(jnp.concatenate(ps, axis=1), vcat))
    o = jnp.concatenate(outs, axis=1)
    o_ref[...] = _layer_norm(ALPHA * x + _dot(o, wo_ref[...]), g_ref[...], b_ref[...])


def _mem_attn_prompt(x, wq, kv, wo, g, b, batch, tm):
    n = x.shape[0]
    per = n // batch // tm
    const = lambda i: (0, 0)
    return pl.pallas_call(
        _mem_attn_prompt_kernel,
        grid=(n // tm,),
        in_specs=[pl.BlockSpec((tm, D_MODEL), lambda i: (i, 0)),
                  pl.BlockSpec(wq.shape, const),
                  pl.BlockSpec((N_MEM, 2 * MEM_W), lambda i: (i // per, 0)),
                  pl.BlockSpec(wo.shape, const),
                  pl.BlockSpec((1, D_MODEL), const),
                  pl.BlockSpec((1, D_MODEL), const)],
        out_specs=pl.BlockSpec((tm, D_MODEL), lambda i: (i, 0)),
        out_shape=jax.ShapeDtypeStruct((n, D_MODEL), F32),
        compiler_params=_cparams("parallel"),
        name="mem_attn_prompt",
    )(x, wq, kv, wo, g.reshape(1, -1), b.reshape(1, -1))


def _mem_attn_sample_kernel(x_ref, wq_ref, k_ref, v_ref, o_ref, q_sc):
    i = pl.program_id(0)

    @pl.when(i == 0)
    def _():
        q_sc[...] = _dot(x_ref[...], wq_ref[...]) * (HEAD_DIM ** -0.5)

    lo = _lo_mask((N_MEM, LANES))
    for bb in range(SAMPLE_BLOCK):
        q = q_sc[pl.ds(i * SAMPLE_BLOCK + bb, 1), :]
        kb = k_ref[bb]
        vb = v_ref[bb]
        prod = kb * q
        outs = []
        for t in range(MEM_HEADS // 2):
            pt = prod[:, t * LANES:(t + 1) * LANES]
            s_a = jnp.sum(jnp.where(lo, pt, 0.0), axis=-1, keepdims=True)
            s_b = jnp.sum(jnp.where(lo, 0.0, pt), axis=-1, keepdims=True)
            e_a = jnp.exp(s_a - jnp.max(s_a, axis=0, keepdims=True))
            e_b = jnp.exp(s_b - jnp.max(s_b, axis=0, keepdims=True))
            p_a = e_a * (1.0 / jnp.sum(e_a, axis=0, keepdims=True))
            p_b = e_b * (1.0 / jnp.sum(e_b, axis=0, keepdims=True))
            vt = vb[:, t * LANES:(t + 1) * LANES]
            outs.append(jnp.sum(jnp.where(lo, p_a, p_b) * vt, axis=0, keepdims=True))
        o_ref[bb:bb + 1, :] = jnp.concatenate(outs, axis=1)


def _mem_attn_sample(layer, x, wq, cache_k, cache_v):
    nd = x.shape[0]
    nb = SAMPLE_BLOCK
    return pl.pallas_call(
        _mem_attn_sample_kernel,
        grid=(nd // nb,),
        in_specs=[pl.BlockSpec((nd, D_MODEL), lambda i: (0, 0)),
                  pl.BlockSpec(wq.shape, lambda i: (0, 0)),
                  pl.BlockSpec((None, nb, N_MEM, MEM_W), lambda i: (layer, i, 0, 0)),
                  pl.BlockSpec((None, nb, N_MEM, MEM_W), lambda i: (layer, i, 0, 0))],
        out_specs=pl.BlockSpec((nb, MEM_W), lambda i: (i, 0)),
        out_shape=jax.ShapeDtypeStruct((nd, MEM_W), F32),
        scratch_shapes=[pltpu.VMEM((nd, MEM_W), F32)],
        compiler_params=_cparams("arbitrary"),
        name="mem_attn_sample",
    )(x, wq, cache_k, cache_v)


def _route(x, wr, br):
    x_hi = x.astype(BF16)
    x_lo = (x - x_hi.astype(F32)).astype(BF16)
    w_hi = wr.astype(BF16)
    w_lo = (wr - w_hi.astype(F32)).astype(BF16)
    dot = functools.partial(jnp.dot, preferred_element_type=F32)
    logit = dot(x_hi, w_hi) + (dot(x_lo, w_hi) + dot(x_hi, w_lo)) + br
    lane_i = lax.broadcasted_iota(jnp.int32, logit.shape, 1)
    lane = lane_i.astype(F32)
    big = float(LANES)
    is_g = lane_i < MOE_GROUPS
    gl = jnp.where(is_g, logit, NEG_INF)
    gmax = jnp.max(gl, axis=-1, keepdims=True)
    g_idx = jnp.min(jnp.where(is_g & (gl == gmax), lane, big), axis=-1, keepdims=True)
    g_gate = 1.0 / jnp.sum(jnp.where(is_g, jnp.exp(gl - gmax), 0.0), axis=-1, keepdims=True)
    e_lane = lane_i - ROUTER_LANE0
    e_group = jnp.right_shift(e_lane, 2).astype(F32)
    sel = (e_lane >= 0) & (e_lane < MOE_EXPERTS) & (e_group == g_idx)
    el = jnp.where(sel, logit, NEG_INF)
    pe = jnp.where(sel, jnp.exp(el - jnp.max(el, axis=-1, keepdims=True)), 0.0)
    prob = pe * (1.0 / jnp.sum(pe, axis=-1, keepdims=True))
    p1 = jnp.max(jnp.where(sel, prob, -1.0), axis=-1, keepdims=True)
    i1 = jnp.min(jnp.where(sel & (prob == p1), lane, big), axis=-1, keepdims=True)
    rest = sel & (lane != i1)
    p2 = jnp.max(jnp.where(rest, prob, -1.0), axis=-1, keepdims=True)
    i2 = jnp.min(jnp.where(rest & (prob == p2), lane, big), axis=-1, keepdims=True)
    inv = 1.0 / (p1 + p2)
    local = jnp.where(lane == i1, p1 * inv, 0.0) + jnp.where(lane == i2, p2 * inv, 0.0)
    return local * g_gate


def _moe_kernel(x_ref, wr_ref, br_ref, win_ref, wout_ref, g_ref, b_ref, o_ref, xb_sc, gate_sc, acc_sc):
    e = pl.program_id(1)

    @pl.when(e == 0)
    def _():
        x = x_ref[...]
        xb_sc[...] = x.astype(BF16)
        gate_sc[...] = _route(x, wr_ref[...], br_ref[...])
        acc_sc[...] = jnp.zeros_like(acc_sc)

    gates = gate_sc[...]
    lane = lax.broadcasted_iota(jnp.int32, gates.shape, 1)
    gate = jnp.sum(jnp.where(lane == e + ROUTER_LANE0, gates, 0.0), axis=-1, keepdims=True)
    h = jnp.dot(xb_sc[...], win_ref[0].astype(BF16), preferred_element_type=F32)
    act = _silu(h[:, 0:D_EXPERT]) * h[:, D_EXPERT:2 * D_EXPERT] * gate
    acc_sc[...] += _dot(act, wout_ref[0])

    @pl.when(e == pl.num_programs(1) - 1)
    def _():
        o_ref[...] = _layer_norm(ALPHA * x_ref[...] + acc_sc[...], g_ref[...], b_ref[...])


def _moe(layer, x, wr, br, w_in, w_out, g, b, tm):
    n = x.shape[0]
    const = lambda i, e: (0, 0)
    return pl.pallas_call(
        _moe_kernel,
        grid=(n // tm, MOE_EXPERTS),
        in_specs=[pl.BlockSpec((tm, D_MODEL), lambda i, e: (i, 0)),
                  pl.BlockSpec((D_MODEL, LANES), const),
                  pl.BlockSpec((1, LANES), const),
                  pl.BlockSpec((None, 1, D_MODEL, 2 * D_EXPERT), lambda i, e: (layer, e, 0, 0)),
                  pl.BlockSpec((None, 1, D_EXPERT, D_MODEL), lambda i, e: (layer, e, 0, 0)),
                  pl.BlockSpec((1, D_MODEL), const),
                  pl.BlockSpec((1, D_MODEL), const)],
        out_specs=pl.BlockSpec((tm, D_MODEL), lambda i, e: (i, 0)),
        out_shape=jax.ShapeDtypeStruct((n, D_MODEL), F32),
        scratch_shapes=[pltpu.VMEM((tm, D_MODEL), BF16), pltpu.VMEM((tm, LANES), F32),
                        pltpu.VMEM((tm, D_MODEL), F32)],
        compiler_params=_cparams("parallel", "arbitrary"),
        name="moe",
    )(x, wr, br, w_in, w_out, g.reshape(1, -1), b.reshape(1, -1))


def _split_even_weight(w):
    offs = np.cumsum([0, 512, 128, 128, 512, 512, 512, 512])
    qa, ka, va, qb, kb, vb, gb = [w[:, offs[j]:offs[j + 1]] for j in range(7)]
    return jnp.concatenate([qa, qb, kb, vb, gb, ka, va], axis=1).astype(BF16)


def _block_diag(blocks):
    h, r, c = blocks.shape
    eye = jnp.eye(h, dtype=blocks.dtype)
    return (eye[:, None, :, None] * blocks[:, :, None, :]).reshape(h * r, h * c)


def _retention_tables():
    log_gamma = jnp.log1p(-jnp.exp2(-5.0 - jnp.arange(RET_HEADS, dtype=jnp.float32)))
    length = WINDOW
    idx = jnp.arange(length, dtype=jnp.float32)
    diff = idx[:, None] - idx[None, :]
    intra = jnp.where(diff[None] >= 0, jnp.exp(jnp.maximum(diff, 0.0)[None] * log_gamma[:, None, None]), 0.0)
    q_decay = jnp.exp((idx[:, None] + 1.0) * log_gamma[None, :])
    k_decay = jnp.exp((length - 1.0 - idx)[:, None] * log_gamma[None, :])
    s_decay = jnp.exp(length * log_gamma)
    widen = lambda a: jnp.repeat(a, HEAD_DIM, axis=-1)
    step_gamma = jnp.exp(1.0 * log_gamma)
    return intra, widen(q_decay), widen(k_decay), widen(s_decay[None, :]), step_gamma


def kernel(x_prompt, x_sample, mem_prompt, cache_swa_k, cache_swa_v, state_ret, cache_mla, page_table, cache_mem_k, cache_mem_v, w_in_even, swa_sink, ret_gn_g, ret_gn_b, w_out_even, w_in_odd, mla_q_norm, mla_kv_norm, w_uq, w_uk, w_uv, w_out_odd, w_mem_q, w_mem_k, w_mem_v, w_mem_o, moe_w_group, moe_b_group, moe_w_router, moe_b_router, moe_w_in, moe_w_out, ln_g, ln_b):
    batch, seq, _ = x_prompt.shape
    nd = x_sample.shape[0]
    n_even = w_in_even.shape[0]
    n_odd = w_in_odd.shape[0]
    tm_p = 512
    tm_moe = 1024

    xp = x_prompt.reshape(batch * seq, D_MODEL)
    xs = x_sample.reshape(nd, D_MODEL)
    mem = mem_prompt.reshape(batch * N_MEM, D_MODEL)

    pos_p = jnp.arange(seq)
    pos_s = jnp.full((nd,), PAST_LEN, dtype=jnp.int32)
    cos64_p, sin64_p = _rope_tables(pos_p, HEAD_DIM)
    cos64_s, sin64_s = _rope_tables(pos_s, HEAD_DIM)
    cos32_p, sin32_p = _rope_tables(pos_p, MLA_ROPE)
    cos32_s, sin32_s = _rope_tables(pos_s, MLA_ROPE)
    intra, qdec, kdec, sdec, step_gamma = _retention_tables()

    cache_k4 = cache_swa_k.reshape(n_even, nd, WINDOW, LANES)
    cache_v4 = cache_swa_v.reshape(n_even, nd, WINDOW, LANES)
    cache_mk = cache_mem_k.reshape(DEPTH, nd, N_MEM, MEM_W)
    cache_mv = cache_mem_v.reshape(DEPTH, nd, N_MEM, MEM_W)

    swa_k_p, swa_v_p, ret_p, mla_p, mem_k_p, mem_v_p = [], [], [], [], [], []
    swa_k_s, swa_v_s, ret_s, mla_s = [], [], [], []
    for l in range(DEPTH):
        i = l // 2
        g0, b0 = ln_g[l, 0], ln_b[l, 0]
        if l % 2 == 0:
            w_in = _split_even_weight(w_in_even[i])
            w_out = w_out_even[i].astype(BF16)
            gng = ret_gn_g[i].reshape(1, -1)
            gnb = ret_gn_b[i].reshape(1, -1)
            qa, qr, kr, vr, g, kava = _even_proj(xp, w_in, cos64_p, sin64_p, tm_p)
            mix, s_fin = _even_mix_prompt(swa_sink[i], qa, kava, qr, kr, vr, g,
                                          intra, qdec, kdec, sdec, gng, gnb, batch)
            last = kava.reshape(batch, seq, 2 * LANES)[:, seq - WINDOW:]
            swa_k_p.append(last[:, :, 0:LANES].reshape(batch, WINDOW, SWA_KV_HEADS, HEAD_DIM))
            swa_v_p.append(last[:, :, LANES:].reshape(batch, WINDOW, SWA_KV_HEADS, HEAD_DIM))
            ret_p.append(s_fin)
            xp = _out_proj_ln(xp, [mix], [w_out], g0, b0, tm_p)
            qa, qr, kr, vr, g, kava = _even_proj(xs, w_in, cos64_s, sin64_s, nd)
            o_swa, gated, ck_new, cv_new, s_new = _even_mix_sample(
                i, step_gamma, swa_sink[i].reshape(SWA_HEADS, 1), qa, kava, cache_k4, cache_v4,
                qr, kr, vr, g, state_ret, ret_gn_g[i], ret_gn_b[i])
            swa_k_s.append(ck_new.reshape(nd, WINDOW, SWA_KV_HEADS, HEAD_DIM))
            swa_v_s.append(cv_new.reshape(nd, WINDOW, SWA_KV_HEADS, HEAD_DIM))
            ret_s.append(s_new)
            xs = _out_proj_ln(xs, [o_swa.reshape(nd, 512), gated.reshape(nd, 512)],
                              [w_out[0:512], w_out[512:1024]], g0, b0, nd)
        else:
            wi = w_in_odd[i]
            w_in = jnp.concatenate([wi, jnp.zeros((D_MODEL, LANES - MLA_ROPE), wi.dtype)], axis=1).astype(BF16)
            uq = w_uq[i].reshape(MLA_Q_RANK, MLA_HEADS, MLA_NOPE + MLA_ROPE)
            wuqn = uq[:, :, 0:MLA_NOPE].reshape(MLA_Q_RANK, MLA_HEADS * MLA_NOPE).astype(BF16)
            wuqr = jnp.concatenate(
                [uq[:, :, MLA_NOPE:], jnp.zeros((MLA_Q_RANK, MLA_HEADS, LANES - MLA_ROPE), uq.dtype)],
                axis=2).reshape(MLA_Q_RANK, MLA_HEADS * LANES).astype(BF16)
            wukbd = _block_diag(jnp.transpose(w_uk[i], (1, 2, 0))).astype(BF16)
            wuvbd = _block_diag(jnp.transpose(w_uv[i], (1, 0, 2))).astype(BF16)
            w_out = w_out_odd[i].astype(BF16)
            q2d, rows, rowsk = _mla_proj(xp, w_in, mla_q_norm[i], mla_kv_norm[i], wuqn, wuqr, wukbd,
                                         cos32_p, sin32_p, tm_p)
            ctx = _mla_flash(q2d.reshape(batch * seq * MLA_HEADS, MLA_QK_PAD), rowsk, batch, seq)
            mla_p.append(rows.reshape(batch, seq, MLA_ROW_W))
            xp = _out_proj_ln(xp, [ctx.reshape(batch * seq, MLA_HEADS * MLA_KV_RANK)], [w_out], g0, b0, tm_p,
                              w_pre=wuvbd)
            q2d, rows, rowsk = _mla_proj(xs, w_in, mla_q_norm[i], mla_kv_norm[i], wuqn, wuqr, wukbd,
                                         cos32_s, sin32_s, nd)
            ctx = _mla_sample(i, page_table, q2d, rowsk, cache_mla)
            mla_s.append(rows.reshape(nd, 1, MLA_ROW_W))
            xs = _out_proj_ln(xs, [ctx.reshape(nd, MLA_HEADS * MLA_KV_RANK)], [w_out], g0, b0, nd,
                              w_pre=wuvbd)
        wq = w_mem_q[l].astype(BF16)
        wo = w_mem_o[l].astype(BF16)
        wkv = jnp.concatenate([w_mem_k[l], w_mem_v[l]], axis=1).astype(BF16)
        kv = _linear(mem, wkv, 256)
        mem_k_p.append(kv[:, 0:MEM_W].reshape(batch, N_MEM, MEM_HEADS, HEAD_DIM))
        mem_v_p.append(kv[:, MEM_W:].reshape(batch, N_MEM, MEM_HEADS, HEAD_DIM))
        xp = _mem_attn_prompt(xp, wq, kv, wo, ln_g[l, 1], ln_b[l, 1], batch, tm_p)
        o_mem = _mem_attn_sample(l, xs, wq, cache_mk, cache_mv)
        xs = _out_proj_ln(xs, [o_mem], [wo], ln_g[l, 1], ln_b[l, 1], nd)
        pad = jnp.zeros((D_MODEL, ROUTER_LANE0 - MOE_GROUPS), F32)
        wr = jnp.concatenate([moe_w_group[l], pad, moe_w_router[l],
                              jnp.zeros((D_MODEL, LANES - ROUTER_LANE0 - MOE_EXPERTS), F32)], axis=1)
        br = jnp.concatenate([moe_b_group[l], jnp.zeros((ROUTER_LANE0 - MOE_GROUPS,), F32), moe_b_router[l],
                              jnp.zeros((LANES - ROUTER_LANE0 - MOE_EXPERTS,), F32)]).reshape(1, LANES)
        xp = _moe(l, xp, wr, br, moe_w_in, moe_w_out, ln_g[l, 2], ln_b[l, 2], tm_moe)
        xs = _moe(l, xs, wr, br, moe_w_in, moe_w_out, ln_g[l, 2], ln_b[l, 2], nd)

    return (xp.reshape(batch, seq, D_MODEL), xs.reshape(nd, 1, D_MODEL),
            jnp.stack(swa_k_p), jnp.stack(swa_v_p), jnp.stack(ret_p), jnp.stack(mla_p),
            jnp.stack(mem_k_p), jnp.stack(mem_v_p), jnp.stack(swa_k_s), jnp.stack(swa_v_s),
            jnp.stack(ret_s), jnp.stack(mla_s))
```
